```python
import math
import jax, jax.numpy as jnp
from jax import lax
import numpy as np

D_MODEL = 1024
BATCH = 2
SEQ = 8192
DEPTH = 2
DEC_BATCH = 16
DEC_SEQ = 2048
PAST_LEN = 128

HEAD_DIM = 64
D_MIX = D_MODEL
CHUNK = 128
A_WIDTH = D_MIX // 4
A_GROUPS = A_WIDTH // HEAD_DIM
A_GROUP_DIM = A_WIDTH // A_GROUPS
B_WIDTH = 3 * D_MIX // 8
B_HEADS = B_WIDTH // HEAD_DIM
C_WIDTH = 3 * D_MIX // 8
C_HEADS = C_WIDTH // HEAD_DIM
C_KV_HEADS = C_HEADS // 3
KV_WIDTH = C_KV_HEADS * HEAD_DIM
IN_DIM = 2 * A_WIDTH + 4 * B_WIDTH + C_WIDTH + 2 * KV_WIDTH
WINDOW = 128
ROT_DIM = HEAD_DIM // 4
ROPE_THETA = 500000.0
RET_THETA = 10000.0
D_FF = 2816
CONV_WIDTH = 3
EPS = 1e-6

kernel_name = 'hybrid_bidir_encoder_two_groups'


def _rmsnorm(x, g):
    xf = x.astype(jnp.float32)
    y = xf * lax.rsqrt(jnp.mean(xf * xf, axis=-1, keepdims=True) + EPS)
    return (y * g.astype(jnp.float32)).astype(x.dtype)


def _rope(x, rot_dim, theta):
    s = x.shape[1]
    half = rot_dim // 2
    freqs = jnp.exp(-math.log(theta) * jnp.arange(half, dtype=jnp.float32) * 2.0 / rot_dim)
    ang = jnp.arange(s, dtype=jnp.float32)[:, None] * freqs[None, :]
    cos = jnp.cos(ang)[:, None, :]
    sin = jnp.sin(ang)[:, None, :]
    x1 = x[..., :half].astype(jnp.float32)
    x2 = x[..., half:rot_dim].astype(jnp.float32)
    rest = x[..., rot_dim:].astype(jnp.float32)
    return jnp.concatenate([x1 * cos - x2 * sin, x1 * sin + x2 * cos, rest], axis=-1)


def _sgu(u, v, norm_g, w_s, b_s):
    bsz, s, _ = u.shape
    n = s // CHUNK
    u = jax.nn.gelu(u)
    v = jax.nn.gelu(v).reshape(bsz, s, A_GROUPS, A_GROUP_DIM)
    v = _rmsnorm(v, norm_g.reshape(A_GROUPS, A_GROUP_DIM))
    vc = v.reshape(bsz, n, CHUNK, A_GROUPS, A_GROUP_DIM)
    gate = jnp.einsum('gij,bnjgd->bnigd', w_s, vc) + b_s.T[None, None, :, :, None]
    return u * gate.reshape(bsz, s, A_WIDTH)


def _retention_dir(q, k, v, log_gamma, strict):
    bsz, _, c, h, dk = q.shape
    dv = v.shape[-1]
    pos = jnp.arange(c, dtype=jnp.float32)
    diff = pos[:, None] - pos[None, :]
    mask = (diff > 0) if strict else (diff >= 0)
    decay_in = jnp.where(mask[None], jnp.exp(jnp.maximum(diff, 0.0)[None] * log_gamma[:, None, None]), 0.0)
    scores = jnp.einsum('bnihd,bnjhd->bnhij', q, k) * decay_in
    y_inner = jnp.einsum('bnhij,bnjhe->bnihe', scores, v)
    k_dec = jnp.exp((c - 1 - pos)[:, None] * log_gamma[None, :])
    kv = jnp.einsum('bnjhd,bnjhe->bnhde', k * k_dec[None, None, :, :, None], v)
    chunk_decay = jnp.exp(c * log_gamma)[:, None, None]

    def step(state, kv_n):
        return chunk_decay * state + kv_n, state

    s0 = jnp.zeros((bsz, h, dk, dv), jnp.float32)
    _, s_prev = lax.scan(step, s0, jnp.moveaxis(kv, 1, 0))
    s_prev = jnp.moveaxis(s_prev, 0, 1)
    q_dec = jnp.exp((pos + 1.0)[:, None] * log_gamma[None, :])
    y_cross = jnp.einsum('bnihd,bnhde->bnihe', q * q_dec[None, None, :, :, None], s_prev)
    return y_inner + y_cross


def _retention(q, k, v, g, dec_f, dec_b):
    bsz, s, h, d = q.shape
    n = s // CHUNK
    q = _rope(q, HEAD_DIM, RET_THETA)
    k = _rope(k, HEAD_DIM, RET_THETA) * (HEAD_DIM ** -0.5)
    v = v.astype(jnp.float32)
    chunk = lambda t: t.reshape(bsz, n, CHUNK, h, d)
    flip = lambda t: chunk(t[:, ::-1])
    lg_f = jax.nn.log_sigmoid(dec_f.astype(jnp.float32))
    lg_b = jax.nn.log_sigmoid(dec_b.astype(jnp.float32))
    y_f = _retention_dir(chunk(q), chunk(k), chunk(v), lg_f, False).reshape(bsz, s, h, d)
    y_b = _retention_dir(flip(q), flip(k), flip(v), lg_b, True).reshape(bsz, s, h, d)[:, ::-1]
    y = y_f + y_b
    mu = jnp.mean(y, axis=-1, keepdims=True)
    var = jnp.mean(jnp.square(y - mu), axis=-1, keepdims=True)
    y = (y - mu) * lax.rsqrt(var + EPS)
    out = jax.nn.silu(g.astype(jnp.float32)) * y
    return out.reshape(bsz, s, B_WIDTH).astype(g.dtype)


def _window_gqa(q, k, v, sink):
    bsz, s, _, _ = q.shape
    n = s // CHUNK
    grp = C_HEADS // C_KV_HEADS
    q = _rope(q, ROT_DIM, ROPE_THETA)
    k = _rope(k, ROT_DIM, ROPE_THETA)
    qb = q.reshape(bsz, n, CHUNK, C_KV_HEADS, grp, HEAD_DIM)

    def band(t):
        tp = jnp.pad(t.astype(jnp.float32), ((0, 0), (CHUNK, CHUNK), (0, 0), (0, 0)))
        tp = tp.reshape(bsz, n + 2, CHUNK, C_KV_HEADS, HEAD_DIM)
        return jnp.concatenate([tp[:, :-2], tp[:, 1:-1], tp[:, 2:]], axis=2)

    kb = band(k)
    vb = band(v)
    sc = jnp.einsum('bnikgd,bnjkd->bnkgij', qb, kb) * (HEAD_DIM ** -0.5)
    blk = jnp.arange(n)[:, None, None]
    qpos = blk * CHUNK + jnp.arange(CHUNK)[None, :, None]
    kpos = (blk - 1) * CHUNK + jnp.arange(3 * CHUNK)[None, None, :]
    mask = (jnp.abs(kpos - qpos) <= WINDOW) & (kpos >= 0) & (kpos < s)
    sc = jnp.where(mask[None, :, None, None], sc, -jnp.inf)
    sk = sink.astype(jnp.float32).reshape(C_KV_HEADS, grp)[:, :, None, None]
    m = jnp.maximum(jnp.max(sc, axis=-1, keepdims=True), sk)
    p = jnp.exp(sc - m)
    denom = jnp.sum(p, axis=-1, keepdims=True) + jnp.exp(sk - m)
    o = jnp.einsum('bnkgij,bnjkd->bnikgd', p / denom, vb)
    return o.reshape(bsz, s, C_WIDTH)


def _token_mixers(h, w_in, sgu_norm, sgu_w, sgu_b, dec_f, dec_b, sink, w_out):
    bsz, s, _ = h.shape
    proj = h @ w_in
    sizes = [A_WIDTH, A_WIDTH, B_WIDTH, B_WIDTH, B_WIDTH, B_WIDTH, C_WIDTH, KV_WIDTH]
    points = list(np.cumsum(sizes))
    a_u, a_v, b_q, b_k, b_v, b_g, c_q, c_k, c_v = jnp.split(proj, points, axis=-1)
    y_a = _sgu(a_u, a_v, sgu_norm, sgu_w, sgu_b)
    hb = lambda t: t.reshape(bsz, s, B_HEADS, HEAD_DIM)
    y_b = _retention(hb(b_q), hb(b_k), hb(b_v), hb(b_g), dec_f, dec_b)
    y_c = _window_gqa(c_q.reshape(bsz, s, C_HEADS, HEAD_DIM),
                      c_k.reshape(bsz, s, C_KV_HEADS, HEAD_DIM),
                      c_v.reshape(bsz, s, C_KV_HEADS, HEAD_DIM), sink)
    y = jnp.concatenate([y_a, y_b, y_c], axis=-1).astype(h.dtype)
    return y @ w_out


def _conv_ffn(h, w_up, conv_w, conv_b, w_down):
    s = h.shape[1]
    up = h @ w_up
    pad = CONV_WIDTH // 2
    upp = jnp.pad(up, ((0, 0), (pad, pad), (0, 0)))
    conv = conv_b[None, None, :]
    for t in range(CONV_WIDTH):
        conv = conv + conv_w[t][None, None, :] * upp[:, t:t + s]
    gate, val = jnp.split(conv, 2, axis=-1)
    return (jax.nn.gelu(gate) * val) @ w_down


def _layer(x, c, w_ada, b_ada, g_pre_mix, g_post_mix, g_pre_ffn, g_post_ffn,
           w_in, sgu_norm, sgu_w, sgu_b, dec_f, dec_b, sink, w_out,
           w_up, conv_w, conv_b, w_down):
    mod = jax.nn.silu(c) @ w_ada + b_ada
    sh1, sc1, gt1, sh2, sc2, gt2 = [m[:, None, :] for m in jnp.split(mod, 6, axis=-1)]
    h = _rmsnorm(x, g_pre_mix) * (1.0 + sc1) + sh1
    y = _token_mixers(h, w_in, sgu_norm, sgu_w, sgu_b, dec_f, dec_b, sink, w_out)
    x = x + gt1 * _rmsnorm(y, g_post_mix)
    h = _rmsnorm(x, g_pre_ffn) * (1.0 + sc2) + sh2
    y = _conv_ffn(h, w_up, conv_w, conv_b, w_down)
    x = x + gt2 * _rmsnorm(y, g_post_ffn)
    return x


def setup_inputs(seed: int = 0) -> dict:
    key = jax.random.key(seed)
    ks = jax.random.split(key, 24)
    nrm = lambda k, shape, scale: scale * jax.random.normal(k, shape, jnp.float32)
    z0 = jnp.asarray(np.log(2.0 ** (5 + np.arange(B_HEADS)) - 1.0).astype(np.float32))
    return {
        'x_prompt': nrm(ks[0], (BATCH, SEQ, D_MODEL), 1.0),
        'x_sample': nrm(ks[1], (DEC_BATCH, DEC_SEQ, D_MODEL), 1.0),
        'c_prompt': nrm(ks[2], (BATCH, D_MODEL), 1.0),
        'c_sample': nrm(ks[3], (DEC_BATCH, D_MODEL), 1.0),
        'w_ada': nrm(ks[4], (DEPTH, D_MODEL, 6 * D_MODEL), D_MODEL ** -0.5),
        'b_ada': nrm(ks[5], (DEPTH, 6 * D_MODEL), 0.02),
        'norm_pre_mix': 1.0 + nrm(ks[6], (DEPTH, D_MODEL), 0.05),
        'norm_post_mix': 1.0 + nrm(ks[7], (DEPTH, D_MODEL), 0.05),
        'norm_pre_ffn': 1.0 + nrm(ks[8], (DEPTH, D_MODEL), 0.05),
        'norm_post_ffn': 1.0 + nrm(ks[9], (DEPTH, D_MODEL), 0.05),
        'w_in': nrm(ks[10], (DEPTH, D_MODEL, IN_DIM), D_MODEL ** -0.5),
        'sgu_norm': 1.0 + nrm(ks[11], (DEPTH, A_WIDTH), 0.05),
        'sgu_w': nrm(ks[12], (DEPTH, A_GROUPS, CHUNK, CHUNK), CHUNK ** -0.5),
        'sgu_b': 1.0 + nrm(ks[13], (DEPTH, A_GROUPS, CHUNK), 0.1),
        'ret_decay_fwd': z0[None, :] + nrm(ks[14], (DEPTH, B_HEADS), 0.05),
        'ret_decay_bwd': z0[None, :] + nrm(ks[15], (DEPTH, B_HEADS), 0.05),
        'attn_sink': nrm(ks[16], (DEPTH, C_HEADS), 0.5),
        'w_out': nrm(ks[17], (DEPTH, D_MIX, D_MODEL), D_MIX ** -0.5),
        'w_up': nrm(ks[18], (DEPTH, D_MODEL, 2 * D_FF), D_MODEL ** -0.5),
        'conv_w': nrm(ks[19], (DEPTH, CONV_WIDTH, 2 * D_FF), CONV_WIDTH ** -0.5),
        'conv_b': nrm(ks[20], (DEPTH, 2 * D_FF), 0.02),
        'w_down': nrm(ks[21], (DEPTH, D_FF, D_MODEL), D_FF ** -0.5),
    }


def reference(x_prompt, x_sample, c_prompt, c_sample, w_ada, b_ada,
              norm_pre_mix, norm_post_mix, norm_pre_ffn, norm_post_ffn,
              w_in, sgu_norm, sgu_w, sgu_b, ret_decay_fwd, ret_decay_bwd, attn_sink,
              w_out, w_up, conv_w, conv_b, w_down):
    y_prompt = x_prompt
    y_sample = x_sample
    for l in range(DEPTH):
        layer_params = (w_ada[l], b_ada[l], norm_pre_mix[l], norm_post_mix[l],
                        norm_pre_ffn[l], norm_post_ffn[l], w_in[l], sgu_norm[l],
                        sgu_w[l], sgu_b[l], ret_decay_fwd[l], ret_decay_bwd[l],
                        attn_sink[l], w_out[l], w_up[l], conv_w[l], conv_b[l], w_down[l])
        y_prompt = _layer(y_prompt, c_prompt, *layer_params)
        y_sample = _layer(y_sample, c_sample, *layer_params)
    return (y_prompt, y_sample)
```

```python
import functools
import math

import numpy as np
import jax
import jax.numpy as jnp
from jax import lax
from jax.experimental import pallas as pl
from jax.experimental.pallas import tpu as pltpu

F32 = jnp.float32
BF16 = jnp.bfloat16

D_MODEL = 1024
HEAD_DIM = 64
CHUNK = 128
LANES = 128
A_WIDTH = 256
B_WIDTH = 384
C_WIDTH = 384
KV_WIDTH = 128
IN_DIM = 2 * A_WIDTH + 4 * B_WIDTH + C_WIDTH + 2 * KV_WIDTH
PRE_DIM = IN_DIM - 2 * KV_WIDTH
B_TILES = B_WIDTH // LANES
C_TILES = C_WIDTH // LANES
A_TILES = A_WIDTH // LANES
WINDOW = 128
ROT_DIM = 16
ROPE_THETA = 500000.0
RET_THETA = 10000.0
D_FF = 2816
FF_TILE = 256
N_FF_TILES = D_FF // FF_TILE
EPS = 1e-6
HALO = 16
NEG = -1e30

O_AU, O_AV = 0, 256
O_BQ, O_BK, O_BV, O_BG = 512, 896, 1280, 1664
O_CQ, O_CKV = 2048, 2432
Y_A, Y_B, Y_C = 0, 256, 640

C_HEAD_PERM = (0, 3, 1, 4, 2, 5)

VMEM_LIMIT = 52 * 1024 * 1024


def _gelu(x):
    c = math.sqrt(2.0 / math.pi)
    return x * (0.5 * (1.0 + jnp.tanh(c * (x + 0.044715 * (x * x * x)))))


def _sigmoid(x):
    return 1.0 / (1.0 + jnp.exp(-x))


def _log_sigmoid(x):
    return jnp.minimum(x, 0.0) - jnp.log1p(jnp.exp(-jnp.abs(x)))


def _lane_lo(shape):
    return lax.broadcasted_iota(jnp.int32, shape, len(shape) - 1) < HEAD_DIM


def _half_mean(y, lo):
    s_lo = jnp.sum(y[:, :HEAD_DIM], axis=-1, keepdims=True)
    s_hi = jnp.sum(y[:, HEAD_DIM:], axis=-1, keepdims=True)
    return jnp.where(lo, s_lo, s_hi) * (1.0 / HEAD_DIM)


def _rope_tile(x, cos, sin_signed, half):
    lane = lax.broadcasted_iota(jnp.int32, x.shape, 1)
    first = (lane & (HEAD_DIM - 1)) < half
    rot = jnp.where(first, pltpu.roll(x, LANES - half, 1), pltpu.roll(x, half, 1))
    return x * cos + rot * sin_signed


def _split_halves(x):
    lo = _lane_lo(x.shape)
    z = jnp.zeros_like(x)
    return jnp.concatenate([jnp.where(lo, x, z), jnp.where(lo, z, x)], axis=0)


def _dot(a, b):
    return jnp.dot(a, b, preferred_element_type=F32)


def _dot_nt(a, b):
    return lax.dot_general(a, b, (((1,), (1,)), ((), ())), preferred_element_type=F32)


def _mod_kernel(c_ref, w_ref, b_ref, o_ref):
    c = c_ref[...]
    a = c * _sigmoid(c)
    w = w_ref[0]
    a_hi = a.astype(BF16)
    a_lo = (a - a_hi.astype(F32)).astype(BF16)
    w_hi = w.astype(BF16)
    w_lo = (w - w_hi.astype(F32)).astype(BF16)
    acc = _dot(a_hi, w_hi) + (_dot(a_hi, w_lo) + _dot(a_lo, w_hi))
    o_ref[0] = acc + b_ref[0]


def _modulation(c_all, w_ada, b_ada):
    depth = w_ada.shape[0]
    nb = c_all.shape[0]
    tn = D_MODEL
    return pl.pallas_call(
        _mod_kernel,
        grid=(depth, 6 * D_MODEL // tn),
        in_specs=[
            pl.BlockSpec((nb, D_MODEL), lambda l, j: (0, 0)),
            pl.BlockSpec((1, D_MODEL, tn), lambda l, j: (l, 0, j)),
            pl.BlockSpec((1, 1, tn), lambda l, j: (l, 0, j)),
        ],
        out_specs=pl.BlockSpec((1, nb, tn), lambda l, j: (l, 0, j)),
        out_shape=jax.ShapeDtypeStruct((depth, nb, 6 * D_MODEL), F32),
        compiler_params=pltpu.CompilerParams(
            dimension_semantics=("arbitrary", "arbitrary"), vmem_limit_bytes=VMEM_LIMIT),
        name="adaln_modulation",
    )(c_all, w_ada, b_ada.reshape(depth, 1, 6 * D_MODEL))


def _inproj_kernel(x_ref, mod_ref, g_ref, w_ref, sgn_ref, cosr_ref, sinr_ref, cosa_ref, sina_ref, dec_ref,
                   pre_ref, ckv_ref, sb_ref, kvf_ref,
                   h_ref, state_ref, kdf_ref, kdb_ref, gamb_ref, *, ts):
    n_chunks = ts // CHUNK
    j = pl.program_id(1)

    @pl.when(j == 0)
    def _init():
        state_ref[...] = jnp.zeros_like(state_ref)
        lg = _log_sigmoid(dec_ref[...])
        pos = lax.broadcasted_iota(jnp.int32, (CHUNK, B_WIDTH), 0).astype(F32)
        kdf = jnp.exp((CHUNK - 1.0 - pos) * lg[0:1, :])
        kdb = jnp.exp(pos * lg[1:2, :])
        for t in range(B_TILES):
            kdf_ref[t] = kdf[:, t * LANES:(t + 1) * LANES].T
            kdb_ref[t] = kdb[:, t * LANES:(t + 1) * LANES].T
        gamb_ref[...] = jnp.exp(float(CHUNK) * lg[1:2, :])

    x = x_ref[0]
    ms = jnp.mean(x * x, axis=-1, keepdims=True)
    hn = (x * lax.rsqrt(ms + EPS)) * g_ref[...]
    h = hn * (1.0 + mod_ref[0, 1:2, :]) + mod_ref[0, 0:1, :]
    h_ref[...] = h.astype(BF16)

    def proj(a, b):
        return _dot(h_ref[...], w_ref[:, a:b])

    lo = _lane_lo((ts, LANES))

    pre_ref[0, :, O_AU:O_AU + A_WIDTH] = _gelu(proj(O_AU, O_AU + A_WIDTH)).astype(BF16)

    av = _gelu(proj(O_AV, O_AV + A_WIDTH))
    for t in range(A_TILES):
        vt = av[:, t * LANES:(t + 1) * LANES]
        msq = _half_mean(vt * vt, lo)
        vn = (vt * lax.rsqrt(msq + EPS)) * sgn_ref[:, t * LANES:(t + 1) * LANES]
        pre_ref[0, :, O_AV + t * LANES:O_AV + (t + 1) * LANES] = vn.astype(BF16)

    cosr = cosr_ref[...]
    sinr = sinr_ref[...]
    bq = proj(O_BQ, O_BQ + B_WIDTH)
    for t in range(B_TILES):
        rq = _rope_tile(bq[:, t * LANES:(t + 1) * LANES], cosr, sinr, HEAD_DIM // 2)
        pre_ref[0, :, O_BQ + t * LANES:O_BQ + (t + 1) * LANES] = rq.astype(BF16)

    bk = proj(O_BK, O_BK + B_WIDTH)
    rk = []
    for t in range(B_TILES):
        rkt = _rope_tile(bk[:, t * LANES:(t + 1) * LANES], cosr, sinr, HEAD_DIM // 2) * (HEAD_DIM ** -0.5)
        pre_ref[0, :, O_BK + t * LANES:O_BK + (t + 1) * LANES] = rkt.astype(BF16)
        rk.append(rkt)

    bv = proj(O_BV, O_BV + B_WIDTH).astype(BF16)
    pre_ref[0, :, O_BV:O_BV + B_WIDTH] = bv

    bg = proj(O_BG, O_BG + B_WIDTH)
    pre_ref[0, :, O_BG:O_BG + B_WIDTH] = (bg * _sigmoid(bg)).astype(BF16)

    cosa = cosa_ref[...]
    sina = sina_ref[...]
    cq = proj(O_CQ, O_CQ + C_WIDTH)
    for t in range(C_TILES):
        rq = _rope_tile(cq[:, t * LANES:(t + 1) * LANES], cosa, sina, ROT_DIM // 2) * (HEAD_DIM ** -0.5)
        pre_ref[0, :, O_CQ + t * LANES:O_CQ + (t + 1) * LANES] = rq.astype(BF16)

    ckv = proj(O_CKV, O_CKV + 2 * KV_WIDTH)
    ckv_ref[0, :, 0:KV_WIDTH] = _rope_tile(ckv[:, 0:KV_WIDTH], cosa, sina, ROT_DIM // 2).astype(BF16)
    ckv_ref[0, :, KV_WIDTH:2 * KV_WIDTH] = ckv[:, KV_WIDTH:2 * KV_WIDTH].astype(BF16)

    ri = lax.broadcasted_iota(jnp.int32, (LANES, LANES), 0) < HEAD_DIM
    ci = lax.broadcasted_iota(jnp.int32, (LANES, LANES), 1) < HEAD_DIM
    same_head = ri == ci
    for c in reversed(range(n_chunks)):
        rows = slice(c * CHUNK, (c + 1) * CHUNK)
        for t in range(B_TILES):
            vt = bv[rows, t * LANES:(t + 1) * LANES]
            rkt_t = rk[t][rows, :].T
            kvf = _dot((rkt_t * kdf_ref[t]).astype(BF16), vt)
            kvb = _dot((rkt_t * kdb_ref[t]).astype(BF16), vt)
            kvf_ref[0, c, t] = jnp.where(same_head, kvf, 0.0)
            sb_ref[0, c, t] = state_ref[t].astype(BF16)
            state_ref[t] = gamb_ref[:, t * LANES:(t + 1) * LANES] * state_ref[t] + jnp.where(same_head, kvb, 0.0)


def _inproj(x, mod, g_pre, w_in, sgu_norm, tabs, dec_rep, *, ts):
    bsz, s, _ = x.shape
    nblk = s // ts
    n = s // CHUNK
    r = ts // CHUNK
    cosr, sinr, cosa, sina = tabs
    rev = lambda b, j: (b, nblk - 1 - j, 0)
    tab_spec = pl.BlockSpec((ts, LANES), lambda b, j: (nblk - 1 - j, 0))
    const2 = lambda b, j: (0, 0)
    return pl.pallas_call(
        functools.partial(_inproj_kernel, ts=ts),
        grid=(bsz, nblk),
        in_specs=[
            pl.BlockSpec((1, ts, D_MODEL), rev),
            pl.BlockSpec((1, 6, D_MODEL), lambda b, j: (b, 0, 0)),
            pl.BlockSpec((1, D_MODEL), const2),
            pl.BlockSpec((D_MODEL, IN_DIM), const2, pipeline_mode=pl.Buffered(1)),
            pl.BlockSpec((1, A_WIDTH), const2),
            tab_spec, tab_spec, tab_spec, tab_spec,
            pl.BlockSpec((2, B_WIDTH), const2),
        ],
        out_specs=[
            pl.BlockSpec((1, ts, PRE_DIM), rev),
            pl.BlockSpec((1, ts, 2 * KV_WIDTH), rev),
            pl.BlockSpec((1, r, B_TILES, LANES, LANES), lambda b, j: (b, nblk - 1 - j, 0, 0, 0)),
            pl.BlockSpec((1, r, B_TILES, LANES, LANES), lambda b, j: (b, nblk - 1 - j, 0, 0, 0)),
        ],
        out_shape=[
            jax.ShapeDtypeStruct((bsz, s, PRE_DIM), BF16),
            jax.ShapeDtypeStruct((bsz, s, 2 * KV_WIDTH), BF16),
            jax.ShapeDtypeStruct((bsz, n, B_TILES, LANES, LANES), BF16),
            jax.ShapeDtypeStruct((bsz, n, B_TILES, LANES, LANES), F32),
        ],
        scratch_shapes=[
            pltpu.VMEM((ts, D_MODEL), BF16),
            pltpu.VMEM((B_TILES, LANES, LANES), F32),
            pltpu.VMEM((B_TILES, LANES, LANES), F32),
            pltpu.VMEM((B_TILES, LANES, LANES), F32),
            pltpu.VMEM((1, B_WIDTH), F32),
        ],
        compiler_params=pltpu.CompilerParams(
            dimension_semantics=("arbitrary", "arbitrary"), vmem_limit_bytes=VMEM_LIMIT),
        name="in_projection",
    )(x, mod, g_pre, w_in, sgu_norm, cosr, sinr, cosa, sina, dec_rep)


def _mixer_kernel(x_ref, mod_ref, g_ref, pre_ref, ckv_ref, ckvp_ref, ckvn_ref, sb_ref, kvf_ref,
                  wout_ref, sguw_ref, sgub_ref, dec_ref, dec128_ref, sink_ref,
                  o_ref,
                  ybuf_ref, band_ref, sf_ref, dtab_ref, qdf_ref, qdb_ref, gamf_ref, *, ts, n_total):
    n_chunks = ts // CHUNK
    j = pl.program_id(1)

    @pl.when(j == 0)
    def _init():
        sf_ref[...] = jnp.zeros_like(sf_ref)
        lg = _log_sigmoid(dec_ref[...])
        pos = lax.broadcasted_iota(jnp.int32, (CHUNK, B_WIDTH), 0).astype(F32)
        qdf_ref[...] = jnp.exp((pos + 1.0) * lg[0:1, :])
        qdb_ref[...] = jnp.exp((float(CHUNK) - pos) * lg[1:2, :])
        gamf_ref[...] = jnp.exp(float(CHUNK) * lg[0:1, :])
        lg128 = _log_sigmoid(dec128_ref[...])
        qi = lax.broadcasted_iota(jnp.int32, (CHUNK, CHUNK), 0)
        kj = lax.broadcasted_iota(jnp.int32, (CHUNK, CHUNK), 1)
        dfw = jnp.maximum(qi - kj, 0).astype(F32)
        dbw = jnp.maximum(kj - qi, 0).astype(F32)
        nh = 2 * B_TILES
        for hh in range(nh):
            fw = jnp.exp(dfw * lg128[hh:hh + 1, :])
            bw = jnp.exp(dbw * lg128[nh + hh:nh + hh + 1, :])
            dtab_ref[hh // 2, (hh % 2) * CHUNK:(hh % 2 + 1) * CHUNK, :] = jnp.where(qi >= kj, fw, bw)

    band_ref[0:CHUNK, :] = ckvp_ref[0]
    band_ref[CHUNK:CHUNK + ts, :] = ckv_ref[0]
    band_ref[CHUNK + ts:2 * CHUNK + ts, :] = ckvn_ref[0]

    lo = _lane_lo((CHUNK, LANES))
    qi3 = lax.broadcasted_iota(jnp.int32, (CHUNK, 3 * CHUNK), 0)
    kj3 = lax.broadcasted_iota(jnp.int32, (CHUNK, 3 * CHUNK), 1)

    def chunk_body(c, carry):
        r0 = pl.multiple_of(c * CHUNK, CHUNK)
        rows = pl.ds(r0, CHUNK)

        for t in range(A_TILES):
            vt = pre_ref[0, rows, O_AV + t * LANES:O_AV + (t + 1) * LANES]
            gate = _dot(sguw_ref[t], _split_halves(vt)) + sgub_ref[t]
            u = pre_ref[0, rows, O_AU + t * LANES:O_AU + (t + 1) * LANES].astype(F32)
            ybuf_ref[rows, Y_A + t * LANES:Y_A + (t + 1) * LANES] = (u * gate).astype(BF16)

        for t in range(B_TILES):
            cs = slice(t * LANES, (t + 1) * LANES)
            qt = pre_ref[0, rows, O_BQ + t * LANES:O_BQ + (t + 1) * LANES]
            kt = pre_ref[0, rows, O_BK + t * LANES:O_BK + (t + 1) * LANES]
            vt = pre_ref[0, rows, O_BV + t * LANES:O_BV + (t + 1) * LANES]
            gt = pre_ref[0, rows, O_BG + t * LANES:O_BG + (t + 1) * LANES].astype(F32)
            sc = _dot_nt(_split_halves(qt), kt)
            p = (sc * dtab_ref[t]).astype(BF16)
            qf = qt.astype(F32)
            lhs = jnp.concatenate(
                [p[0:CHUNK], p[CHUNK:2 * CHUNK],
                 (qf * qdf_ref[:, cs]).astype(BF16), (qf * qdb_ref[:, cs]).astype(BF16)], axis=1)
            rhs = jnp.concatenate([_split_halves(vt), sf_ref[t].astype(BF16), sb_ref[0, c, t]], axis=0)
            y = _dot(lhs, rhs)
            mu = _half_mean(y, lo)
            yc = y - mu
            var = _half_mean(yc * yc, lo)
            out = gt * (yc * lax.rsqrt(var + EPS))
            ybuf_ref[rows, Y_B + t * LANES:Y_B + (t + 1) * LANES] = out.astype(BF16)
            sf_ref[t] = gamf_ref[:, cs] * sf_ref[t] + kvf_ref[0, c, t]

        gc = j * n_chunks + c
        lo_lim = jnp.where(gc > 0, 0, CHUNK)
        hi_lim = jnp.where(gc < n_total - 1, 3 * CHUNK, 2 * CHUNK)
        allowed = ((kj3 >= jnp.maximum(qi3, lo_lim))
                   & (kj3 <= jnp.minimum(qi3 + 2 * WINDOW, hi_lim - 1)))
        bias = jnp.where(allowed, 0.0, NEG)
        kb = band_ref[pl.ds(r0, 3 * CHUNK), 0:KV_WIDTH]
        vb = band_ref[pl.ds(r0, 3 * CHUNK), KV_WIDTH:2 * KV_WIDTH]
        qs = jnp.concatenate(
            [_split_halves(pre_ref[0, rows, O_CQ + t * LANES:O_CQ + (t + 1) * LANES]) for t in range(C_TILES)],
            axis=0)
        sc = _dot_nt(qs, kb)
        ps = []
        rinv = []
        for hb in range(2 * C_TILES):
            sch = sc[hb * CHUNK:(hb + 1) * CHUNK, :] + bias
            sk = sink_ref[hb]
            m = jnp.maximum(jnp.max(sch, axis=-1, keepdims=True), sk)
            pe = jnp.exp(sch - m)
            den = jnp.sum(pe, axis=-1, keepdims=True) + jnp.exp(sk - m)
            ps.append(pe.astype(BF16))
            rinv.append(1.0 / den)
        lhs = jnp.concatenate(
            [jnp.concatenate([ps[2 * t], ps[2 * t + 1]], axis=1) for t in range(C_TILES)], axis=0)
        o = _dot(lhs, _split_halves(vb))
        for t in range(C_TILES):
            ot = o[t * CHUNK:(t + 1) * CHUNK, :] * jnp.where(lo, rinv[2 * t], rinv[2 * t + 1])
            ybuf_ref[rows, Y_C + t * LANES:Y_C + (t + 1) * LANES] = ot.astype(BF16)
        return carry

    lax.fori_loop(0, n_chunks, chunk_body, 0)

    y = _dot(ybuf_ref[...], wout_ref[...])
    ms = jnp.mean(y * y, axis=-1, keepdims=True)
    yn = (y * lax.rsqrt(ms + EPS)) * g_ref[...]
    o_ref[0] = x_ref[0] + mod_ref[0, 2:3, :] * yn


def _mixer(x, mod, g_post, pre, ckv, sb, kvf, w_out, sgu_w2, sgu_b2, dec_rep, dec128, sink_p, *, ts):
    bsz, s, _ = x.shape
    nblk = s // ts
    n = s // CHUNK
    r = ts // CHUNK
    fwd = lambda b, j: (b, j, 0)
    const2 = lambda b, j: (0, 0)
    const3 = lambda b, j: (0, 0, 0)
    return pl.pallas_call(
        functools.partial(_mixer_kernel, ts=ts, n_total=n),
        grid=(bsz, nblk),
        in_specs=[
            pl.BlockSpec((1, ts, D_MODEL), fwd),
            pl.BlockSpec((1, 6, D_MODEL), lambda b, j: (b, 0, 0)),
            pl.BlockSpec((1, D_MODEL), const2),
            pl.BlockSpec((1, ts, PRE_DIM), fwd),
            pl.BlockSpec((1, ts, 2 * KV_WIDTH), fwd),
            pl.BlockSpec((1, CHUNK, 2 * KV_WIDTH), lambda b, j: (b, jnp.maximum(j * r - 1, 0), 0)),
            pl.BlockSpec((1, CHUNK, 2 * KV_WIDTH), lambda b, j: (b, jnp.minimum(j * r + r, n - 1), 0)),
            pl.BlockSpec((1, r, B_TILES, LANES, LANES), lambda b, j: (b, j, 0, 0, 0)),
            pl.BlockSpec((1, r, B_TILES, LANES, LANES), lambda b, j: (b, j, 0, 0, 0)),
            pl.BlockSpec((D_MODEL, D_MODEL), const2, pipeline_mode=pl.Buffered(1)),
            pl.BlockSpec((A_TILES, CHUNK, 2 * CHUNK), const3),
            pl.BlockSpec((A_TILES, CHUNK, LANES), const3),
            pl.BlockSpec((2, B_WIDTH), const2),
            pl.BlockSpec((4 * B_TILES, LANES), const2),
            pl.BlockSpec(memory_space=pltpu.SMEM),
        ],
        out_specs=pl.BlockSpec((1, ts, D_MODEL), fwd),
        out_shape=jax.ShapeDtypeStruct((bsz, s, D_MODEL), F32),
        scratch_shapes=[
            pltpu.VMEM((ts, D_MODEL), BF16),
            pltpu.VMEM((ts + 2 * CHUNK, 2 * KV_WIDTH), BF16),
            pltpu.VMEM((B_TILES, LANES, LANES), F32),
            pltpu.VMEM((B_TILES, 2 * CHUNK, CHUNK), F32),
            pltpu.VMEM((CHUNK, B_WIDTH), F32),
            pltpu.VMEM((CHUNK, B_WIDTH), F32),
            pltpu.VMEM((1, B_WIDTH), F32),
        ],
        compiler_params=pltpu.CompilerParams(
            dimension_semantics=("arbitrary", "arbitrary"), vmem_limit_bytes=VMEM_LIMIT),
        name="token_mixers",
    )(x, mod, g_post, pre, ckv, ckv, ckv, sb, kvf, w_out, sgu_w2, sgu_b2, dec_rep, dec128, sink_p)


def _ffn_kernel(x_ref, xp_ref, xn_ref, mod_ref, gpre_ref, gpost_ref, wu_ref, cp_ref, wd_ref,
                o_ref, hbuf_ref, acc_ref, *, ts, nblk):
    j = pl.program_id(1)
    sh2 = mod_ref[0, 3:4, :]
    sc2 = mod_ref[0, 4:5, :]

    def hnorm(x):
        ms = jnp.mean(x * x, axis=-1, keepdims=True)
        return ((x * lax.rsqrt(ms + EPS)) * gpre_ref[...]) * (1.0 + sc2) + sh2

    x = x_ref[0]
    keep_prev = jnp.where(j > 0, 1.0, 0.0)
    keep_next = jnp.where(j < nblk - 1, 1.0, 0.0)
    hbuf_ref[0:HALO, :] = (hnorm(xp_ref[0]) * keep_prev).astype(BF16)
    hbuf_ref[HALO:HALO + ts, :] = hnorm(x).astype(BF16)
    hbuf_ref[HALO + ts:2 * HALO + ts, :] = (hnorm(xn_ref[0]) * keep_next).astype(BF16)
    acc_ref[...] = jnp.zeros_like(acc_ref)
    rows_ext = ts + 2 * HALO

    def tile_body(k, carry):
        up = _dot(hbuf_ref[...], wu_ref[k])
        cp = cp_ref[k]
        prev = pltpu.roll(up, 1, 0)[HALO:HALO + ts, :]
        nxt = pltpu.roll(up, rows_ext - 1, 0)[HALO:HALO + ts, :]
        cur = up[HALO:HALO + ts, :]
        conv = cp[3:4, :] + cp[0:1, :] * prev + cp[1:2, :] * cur + cp[2:3, :] * nxt
        act = _gelu(conv[:, 0:FF_TILE]) * conv[:, FF_TILE:2 * FF_TILE]
        acc_ref[...] += _dot(act.astype(BF16), wd_ref[k])
        return carry

    lax.fori_loop(0, N_FF_TILES, tile_body, 0)

    y = acc_ref[...]
    ms = jnp.mean(y * y, axis=-1, keepdims=True)
    yn = (y * lax.rsqrt(ms + EPS)) * gpost_ref[...]
    o_ref[0] = x + mod_ref[0, 5:6, :] * yn


def _ffn(x, mod, g_pre, g_post, wu_t, cp_t, wd_t, *, ts):
    bsz, s, _ = x.shape
    nblk = s // ts
    hb = ts // HALO
    nh = s // HALO
    fwd = lambda b, j: (b, j, 0)
    const2 = lambda b, j: (0, 0)
    const3 = lambda b, j: (0, 0, 0)
    return pl.pallas_call(
        functools.partial(_ffn_kernel, ts=ts, nblk=nblk),
        grid=(bsz, nblk),
        in_specs=[
            pl.BlockSpec((1, ts, D_MODEL), fwd),
            pl.BlockSpec((1, HALO, D_MODEL), lambda b, j: (b, jnp.maximum(j * hb - 1, 0), 0)),
            pl.BlockSpec((1, HALO, D_MODEL), lambda b, j: (b, jnp.minimum((j + 1) * hb, nh - 1), 0)),
            pl.BlockSpec((1, 6, D_MODEL), lambda b, j: (b, 0, 0)),
            pl.BlockSpec((1, D_MODEL), const2),
            pl.BlockSpec((1, D_MODEL), const2),
            pl.BlockSpec((N_FF_TILES, D_MODEL, 2 * FF_TILE), const3, pipeline_mode=pl.Buffered(1)),
            pl.BlockSpec((N_FF_TILES, 4, 2 * FF_TILE), const3),
            pl.BlockSpec((N_FF_TILES, FF_TILE, D_MODEL), const3, pipeline_mode=pl.Buffered(1)),
        ],
        out_specs=pl.BlockSpec((1, ts, D_MODEL), fwd),
        out_shape=jax.ShapeDtypeStruct((bsz, s, D_MODEL), F32),
        scratch_shapes=[
            pltpu.VMEM((ts + 2 * HALO, D_MODEL), BF16),
            pltpu.VMEM((ts, D_MODEL), F32),
        ],
        compiler_params=pltpu.CompilerParams(
            dimension_semantics=("arbitrary", "arbitrary"), vmem_limit_bytes=VMEM_LIMIT),
        name="conv_ffn",
    )(x, x, x, mod, g_pre, g_post, wu_t, cp_t, wd_t)


def _rope_tables(s):
    pos = jnp.arange(s, dtype=F32)[:, None]
    d = np.arange(LANES) % HEAD_DIM

    half = HEAD_DIM // 2
    fr = jnp.exp(-math.log(RET_THETA) * jnp.arange(half, dtype=F32) * 2.0 / HEAD_DIM)
    ang = pos * fr[d % half][None, :]
    sign = jnp.asarray(np.where(d < half, -1.0, 1.0), F32)[None, :]
    cos_r = jnp.cos(ang)
    sin_r = jnp.sin(ang) * sign

    half = ROT_DIM // 2
    fa = jnp.exp(-math.log(ROPE_THETA) * jnp.arange(half, dtype=F32) * 2.0 / ROT_DIM)
    ang = pos * fa[d % half][None, :]
    rot = jnp.asarray(d < ROT_DIM)[None, :]
    sign = jnp.asarray(np.where(d < half, -1.0, np.where(d < ROT_DIM, 1.0, 0.0)), F32)[None, :]
    cos_a = jnp.where(rot, jnp.cos(ang), 1.0)
    sin_a = jnp.sin(ang) * sign
    return cos_r, sin_r, cos_a, sin_a


def _prep_layer(w_in, sgu_w, sgu_b, dec_f, dec_b, sink, w_out, w_up, conv_w, conv_b, w_down):
    perm = np.asarray(C_HEAD_PERM)
    cq = w_in[:, O_CQ:O_CQ + C_WIDTH].reshape(D_MODEL, 6, HEAD_DIM)[:, perm].reshape(D_MODEL, C_WIDTH)
    w_in_p = jnp.concatenate([w_in[:, :O_CQ], cq, w_in[:, O_CQ + C_WIDTH:]], axis=1).astype(BF16)
    wo_c = w_out[Y_C:].reshape(6, HEAD_DIM, D_MODEL)[perm].reshape(C_WIDTH, D_MODEL)
    w_out_p = jnp.concatenate([w_out[:Y_C], wo_c], axis=0).astype(BF16)
    sgu_w2 = jnp.concatenate([sgu_w[0::2], sgu_w[1::2]], axis=2).astype(BF16)
    sgu_b2 = jnp.repeat(sgu_b.reshape(A_TILES, 2, CHUNK).transpose(0, 2, 1), HEAD_DIM, axis=2)
    dec_rep = jnp.stack([jnp.repeat(dec_f, HEAD_DIM), jnp.repeat(dec_b, HEAD_DIM)]).astype(F32)
    dec128 = jnp.broadcast_to(jnp.concatenate([dec_f, dec_b])[:, None], (4 * B_TILES, LANES)).astype(F32)
    sink_p = sink[perm].astype(F32)
    wu_g = w_up[:, :D_FF].reshape(D_MODEL, N_FF_TILES, FF_TILE)
    wu_v = w_up[:, D_FF:].reshape(D_MODEL, N_FF_TILES, FF_TILE)
    wu_t = jnp.concatenate([wu_g, wu_v], axis=2).transpose(1, 0, 2).astype(BF16)
    cpar = jnp.concatenate([conv_w, conv_b[None, :]], axis=0)
    cp_g = cpar[:, :D_FF].reshape(4, N_FF_TILES, FF_TILE)
    cp_v = cpar[:, D_FF:].reshape(4, N_FF_TILES, FF_TILE)
    cp_t = jnp.concatenate([cp_g, cp_v], axis=2).transpose(1, 0, 2)
    wd_t = w_down.reshape(N_FF_TILES, FF_TILE, D_MODEL).astype(BF16)
    return dict(w_in=w_in_p, w_out=w_out_p, sgu_w2=sgu_w2, sgu_b2=sgu_b2, dec_rep=dec_rep, dec128=dec128,
                sink=sink_p, wu=wu_t, cp=cp_t, wd=wd_t)


def _layer(x, mod, tabs, p, g_pre_mix, g_post_mix, g_pre_ffn, g_post_ffn, sgu_norm, *, ts):
    pre, ckv, sb, kvf = _inproj(x, mod, g_pre_mix, p["w_in"], sgu_norm, tabs, p["dec_rep"], ts=ts)
    x = _mixer(x, mod, g_post_mix, pre, ckv, sb, kvf, p["w_out"], p["sgu_w2"], p["sgu_b2"],
               p["dec_rep"], p["dec128"], p["sink"], ts=ts)
    return _ffn(x, mod, g_pre_ffn, g_post_ffn, p["wu"], p["cp"], p["wd"], ts=ts)


def _block_rows(s):
    return min(512, s)


def kernel(x_prompt, x_sample, c_prompt, c_sample, w_ada, b_ada, norm_pre_mix, norm_post_mix, norm_pre_ffn,
           norm_post_ffn, w_in, sgu_norm, sgu_w, sgu_b, ret_decay_fwd, ret_decay_bwd, attn_sink, w_out, w_up,
           conv_w, conv_b, w_down):
    depth = w_ada.shape[0]
    nbp = x_prompt.shape[0]
    c_all = jnp.concatenate([c_prompt, c_sample], axis=0)
    mod = _modulation(c_all, w_ada, b_ada).reshape(depth, c_all.shape[0], 6, D_MODEL)
    tabs_p = _rope_tables(x_prompt.shape[1])
    tabs_s = _rope_tables(x_sample.shape[1])
    row = lambda v: v.reshape(1, -1)
    y_p, y_s = x_prompt, x_sample
    for l in range(depth):
        p = _prep_layer(w_in[l], sgu_w[l], sgu_b[l], ret_decay_fwd[l], ret_decay_bwd[l], attn_sink[l],
                        w_out[l], w_up[l], conv_w[l], conv_b[l], w_down[l])
        norms = (row(norm_pre_mix[l]), row(norm_post_mix[l]), row(norm_pre_ffn[l]), row(norm_post_ffn[l]),
                 row(sgu_norm[l]))
        y_p = _layer(y_p, mod[l, :nbp], tabs_p, p, *norms, ts=_block_rows(y_p.shape[1]))
        y_s = _layer(y_s, mod[l, nbp:], tabs_s, p, *norms, ts=_block_rows(y_s.shape[1]))
    return (y_p, y_s)
```
